```python
import math
import jax
import jax.numpy as jnp
from jax import lax
import numpy as np

D_MODEL = 1024
BATCH = 4
SEQ = 4096
DEPTH = 2
DEC_BATCH = 2
DEC_SEQ = 16384
PAST_LEN = 128

HEAD_DIM = 64
N_DIFF_HEADS = 4
DIFF_WIDTH = N_DIFF_HEADS * 2 * HEAD_DIM
N_DIL_HEADS = 8
DIL_WIDTH = N_DIL_HEADS * HEAD_DIM
MIX_WIDTH = DIFF_WIDTH + DIL_WIDTH
IN_WIDTH = 3 * DIFF_WIDTH + 3 * DIL_WIDTH
DIL_PATTERNS = ((128, 1), (512, 4), (2048, 16))
ROPE_THETA = 10000.0
N_MEM = 256
N_CROSS_HEADS = 4
CROSS_HEAD_DIM = D_MODEL // N_CROSS_HEADS
D_FF = 2816
CONV_WIDTH = 3
Q_BLOCK = 128
DIL_Q_BLOCK = 64
RMS_EPS = 1e-6

kernel_name = "hybrid_diff_dilated_encoder"


def rmsnorm(x, g):
    xf = x.astype(jnp.float32)
    y = xf * lax.rsqrt(jnp.mean(xf * xf, axis=-1, keepdims=True) + RMS_EPS)
    return (y * g.astype(jnp.float32)).astype(x.dtype)


def rope_tables(seq, dim):
    inv = 1.0 / (ROPE_THETA ** (jnp.arange(0, dim, 2, dtype=jnp.float32) / dim))
    ang = jnp.arange(seq, dtype=jnp.float32)[:, None] * inv[None, :]
    return jnp.cos(ang), jnp.sin(ang)


def apply_rope(x, cos, sin):
    x1, x2 = jnp.split(x.astype(jnp.float32), 2, axis=-1)
    c = cos[:, None, :]
    s = sin[:, None, :]
    return jnp.concatenate([x1 * c - x2 * s, x2 * c + x1 * s], axis=-1).astype(x.dtype)


def diff_attention(q, k, v, lam, sub_g, lambda_init):
    B, S, H, _, d = q.shape
    nb = S // Q_BLOCK
    scale = d ** -0.5
    qb = q.reshape(B, nb, Q_BLOCK, H, 2, d).transpose(1, 0, 2, 3, 4, 5)

    def block(qblk):
        s = jnp.einsum('bqhcd,bkhcd->bhcqk', qblk, k).astype(jnp.float32) * scale
        p = jax.nn.softmax(s, axis=-1)
        w = p[:, :, 0] - lam * p[:, :, 1]
        return jnp.einsum('bhqk,bkhe->bqhe', w.astype(v.dtype), v)

    o = lax.map(block, qb)
    o = o.transpose(1, 0, 2, 3, 4).reshape(B, S, H, 2 * d)
    o = rmsnorm(o, sub_g) * (1.0 - lambda_init)
    return o.reshape(B, S, H * 2 * d)


def dilated_offsets():
    sizes = [2 * (w // (2 * r)) + 1 for (w, r) in DIL_PATTERNS]
    J = max(sizes)
    offs = np.zeros((len(DIL_PATTERNS), J), np.int32)
    valid = np.zeros((len(DIL_PATTERNS), J), bool)
    for g, (w, r) in enumerate(DIL_PATTERNS):
        n = w // (2 * r)
        o = r * np.arange(-n, n + 1)
        offs[g, :o.shape[0]] = o
        valid[g, :o.shape[0]] = True
    return offs, valid


def dilated_attention(q, k, v):
    B, S, H, d = q.shape
    offs_np, valid_np = dilated_offsets()
    reach = int(np.abs(offs_np).max())
    offs = jnp.asarray(offs_np)
    valid = jnp.asarray(valid_np)
    kp = jnp.pad(k, ((0, 0), (reach, reach), (0, 0), (0, 0)))
    vp = jnp.pad(v, ((0, 0), (reach, reach), (0, 0), (0, 0)))
    nb = S // DIL_Q_BLOCK
    scale = d ** -0.5
    qb = q.reshape(B, nb, DIL_Q_BLOCK, H, d).transpose(1, 0, 2, 3, 4)
    starts = jnp.arange(nb, dtype=jnp.int32) * DIL_Q_BLOCK

    def block(args):
        qblk, start = args
        pos = start + jnp.arange(DIL_Q_BLOCK, dtype=jnp.int32)
        kpos = pos[:, None, None] + offs[None]
        ok = valid[None] & (kpos >= 0) & (kpos < S)
        idx = kpos + reach
        kg = jnp.take(kp, idx, axis=1)
        vg = jnp.take(vp, idx, axis=1)
        s = jnp.einsum('bqhd,bqgjhd->bqghj', qblk, kg).astype(jnp.float32) * scale
        s = jnp.where(ok[None, :, :, None, :], s, -jnp.inf)
        lse = jax.nn.logsumexp(s, axis=-1)
        p = jnp.exp(s - lse[..., None])
        o = jnp.einsum('bqghj,bqgjhd->bqghd', p.astype(v.dtype), vg)
        alpha = jax.nn.softmax(lse, axis=2)
        return jnp.einsum('bqgh,bqghd->bqhd', alpha.astype(o.dtype), o)

    o = lax.map(block, (qb, starts))
    return o.transpose(1, 0, 2, 3, 4).reshape(B, S, H * d)


def memory_cross_attention(h, mem_n, w_q, w_kv, w_o):
    B, S, _ = h.shape
    M = mem_n.shape[1]
    q = (h @ w_q).reshape(B, S, N_CROSS_HEADS, CROSS_HEAD_DIM)
    kv = (mem_n @ w_kv).reshape(B, M, 2, N_CROSS_HEADS, CROSS_HEAD_DIM)
    k = kv[:, :, 0]
    v = kv[:, :, 1]
    s = jnp.einsum('bqhd,bmhd->bhqm', q, k).astype(jnp.float32) * (CROSS_HEAD_DIM ** -0.5)
    p = jax.nn.softmax(s, axis=-1)
    o = jnp.einsum('bhqm,bmhd->bqhd', p.astype(v.dtype), v).reshape(B, S, D_MODEL)
    return o @ w_o


def conv_gated_mlp(h, w_up, conv_w, conv_b, w_down):
    S = h.shape[1]
    u = h @ w_up
    half = CONV_WIDTH // 2
    up = jnp.pad(u, ((0, 0), (half, half), (0, 0)))
    c = conv_b
    for t in range(CONV_WIDTH):
        c = c + up[:, t:t + S] * conv_w[t]
    gate, val = jnp.split(c, 2, axis=-1)
    return (jax.nn.gelu(gate, approximate=True) * val) @ w_down


def encoder_trunk(x, mem, p):
    B, S, _ = x.shape
    cos, sin = rope_tables(S, HEAD_DIM)
    splits = np.cumsum([DIFF_WIDTH, DIFF_WIDTH, DIFF_WIDTH, DIL_WIDTH, DIL_WIDTH]).tolist()
    for l in range(DEPTH):
        lambda_init = 0.8 - 0.6 * math.exp(-0.3 * l)
        h = rmsnorm(x, p['mix_pre_g'][l])
        qa, ka, va, qd, kd, vd = jnp.split(h @ p['w_in'][l], splits, axis=-1)
        qa = apply_rope(qa.reshape(B, S, 2 * N_DIFF_HEADS, HEAD_DIM), cos, sin).reshape(B, S, N_DIFF_HEADS, 2, HEAD_DIM)
        ka = apply_rope(ka.reshape(B, S, 2 * N_DIFF_HEADS, HEAD_DIM), cos, sin).reshape(B, S, N_DIFF_HEADS, 2, HEAD_DIM)
        va = va.reshape(B, S, N_DIFF_HEADS, 2 * HEAD_DIM)
        lam = (jnp.exp(jnp.sum(p['lambda_q1'][l].astype(jnp.float32) * p['lambda_k1'][l].astype(jnp.float32)))
               - jnp.exp(jnp.sum(p['lambda_q2'][l].astype(jnp.float32) * p['lambda_k2'][l].astype(jnp.float32)))
               + lambda_init)
        oa = diff_attention(qa, ka, va, lam, p['diff_subln_g'][l], lambda_init)
        qd = apply_rope(qd.reshape(B, S, N_DIL_HEADS, HEAD_DIM), cos, sin)
        kd = apply_rope(kd.reshape(B, S, N_DIL_HEADS, HEAD_DIM), cos, sin)
        vd = vd.reshape(B, S, N_DIL_HEADS, HEAD_DIM)
        od = dilated_attention(qd, kd, vd)
        mix = jnp.concatenate([oa, od], axis=-1) @ p['w_out'][l]
        x = x + rmsnorm(mix, p['mix_post_g'][l])
        h = rmsnorm(x, p['xattn_pre_g'][l])
        mem_n = rmsnorm(mem, p['mem_norm_g'][l])
        c = memory_cross_attention(h, mem_n, p['w_xq'][l], p['w_xkv'][l], p['w_xo'][l])
        x = x + rmsnorm(c, p['xattn_post_g'][l])
        h = rmsnorm(x, p['ffn_pre_g'][l])
        f = conv_gated_mlp(h, p['w_up'][l], p['conv_w'][l], p['conv_b'][l], p['w_down'][l])
        x = x + rmsnorm(f, p['ffn_post_g'][l])
    return x


def setup_inputs(seed: int = 0) -> dict:
    key = jax.random.key(seed)
    ks = jax.random.split(key, 32)
    f32 = jnp.float32

    def nrm(k, shape, scale):
        return jax.random.normal(k, shape, f32) * scale

    def gain(k):
        return 1.0 + 0.1 * jax.random.normal(k, (DEPTH, D_MODEL), f32)

    return {
        'x_prompt': nrm(ks[0], (BATCH, SEQ, D_MODEL), 1.0),
        'x_sample': nrm(ks[1], (DEC_BATCH, DEC_SEQ, D_MODEL), 1.0),
        'mem_prompt': nrm(ks[2], (BATCH, N_MEM, D_MODEL), 1.0),
        'mem_sample': nrm(ks[3], (DEC_BATCH, N_MEM, D_MODEL), 1.0),
        'w_in': nrm(ks[4], (DEPTH, D_MODEL, IN_WIDTH), D_MODEL ** -0.5),
        'w_out': nrm(ks[5], (DEPTH, MIX_WIDTH, D_MODEL), MIX_WIDTH ** -0.5),
        'lambda_q1': nrm(ks[6], (DEPTH, HEAD_DIM), 0.1),
        'lambda_k1': nrm(ks[7], (DEPTH, HEAD_DIM), 0.1),
        'lambda_q2': nrm(ks[8], (DEPTH, HEAD_DIM), 0.1),
        'lambda_k2': nrm(ks[9], (DEPTH, HEAD_DIM), 0.1),
        'diff_subln_g': 1.0 + 0.1 * jax.random.normal(ks[10], (DEPTH, 2 * HEAD_DIM), f32),
        'w_xq': nrm(ks[11], (DEPTH, D_MODEL, D_MODEL), D_MODEL ** -0.5),
        'w_xkv': nrm(ks[12], (DEPTH, D_MODEL, 2 * D_MODEL), D_MODEL ** -0.5),
        'w_xo': nrm(ks[13], (DEPTH, D_MODEL, D_MODEL), D_MODEL ** -0.5),
        'w_up': nrm(ks[14], (DEPTH, D_MODEL, 2 * D_FF), D_MODEL ** -0.5),
        'conv_w': nrm(ks[15], (DEPTH, CONV_WIDTH, 2 * D_FF), CONV_WIDTH ** -0.5),
        'conv_b': nrm(ks[16], (DEPTH, 2 * D_FF), 0.01),
        'w_down': nrm(ks[17], (DEPTH, D_FF, D_MODEL), D_FF ** -0.5),
        'mix_pre_g': gain(ks[18]),
        'mix_post_g': gain(ks[19]),
        'mem_norm_g': gain(ks[20]),
        'xattn_pre_g': gain(ks[21]),
        'xattn_post_g': gain(ks[22]),
        'ffn_pre_g': gain(ks[23]),
        'ffn_post_g': gain(ks[24]),
    }


def reference(x_prompt, x_sample, mem_prompt, mem_sample, w_in, w_out, lambda_q1, lambda_k1,
              lambda_q2, lambda_k2, diff_subln_g, w_xq, w_xkv, w_xo, w_up, conv_w, conv_b, w_down,
              mix_pre_g, mix_post_g, mem_norm_g, xattn_pre_g, xattn_post_g, ffn_pre_g, ffn_post_g):
    params = {
        'w_in': w_in, 'w_out': w_out,
        'lambda_q1': lambda_q1, 'lambda_k1': lambda_k1,
        'lambda_q2': lambda_q2, 'lambda_k2': lambda_k2,
        'diff_subln_g': diff_subln_g,
        'w_xq': w_xq, 'w_xkv': w_xkv, 'w_xo': w_xo,
        'w_up': w_up, 'conv_w': conv_w, 'conv_b': conv_b, 'w_down': w_down,
        'mix_pre_g': mix_pre_g, 'mix_post_g': mix_post_g, 'mem_norm_g': mem_norm_g,
        'xattn_pre_g': xattn_pre_g, 'xattn_post_g': xattn_post_g,
        'ffn_pre_g': ffn_pre_g, 'ffn_post_g': ffn_post_g,
    }
    y_prompt = encoder_trunk(x_prompt, mem_prompt, params)
    y_sample = encoder_trunk(x_sample, mem_sample, params)
    return (y_prompt, y_sample)
```

```python
import functools
import math

import jax
import jax.numpy as jnp
from jax import lax
from jax.experimental import pallas as pl
from jax.experimental.pallas import tpu as pltpu

HEAD_DIM = 64
N_DIFF_HEADS = 4
DIFF_WIDTH = N_DIFF_HEADS * 2 * HEAD_DIM
N_DIL_HEADS = 8
DIL_WIDTH = N_DIL_HEADS * HEAD_DIM
DIL_PATTERNS = ((128, 1), (512, 4), (2048, 16))
DIL_HALF = 64
ROPE_THETA = 10000.0
N_CROSS_HEADS = 4
CONV_WIDTH = 3
RMS_EPS = 1e-6
LOG2E = 1.4426950408889634

LANES = 128
ROW_BLOCK = 512
DIL_Q = 128
CONV_HALO = 8
FFN_CHUNK = 256
NEG_BIG = -1e30
VMEM_LIMIT = 56 * 1024 * 1024

_f32 = jnp.float32
_bf16 = jnp.bfloat16


def _rms(x, g):
    return x * lax.rsqrt(jnp.mean(x * x, axis=-1, keepdims=True) + RMS_EPS) * g


def _dot(a, b):
    return jnp.dot(a, b, preferred_element_type=_f32)


def _dot_nt(a, b):
    return lax.dot_general(a, b, (((1,), (1,)), ((), ())), preferred_element_type=_f32)


def _params(*sem):
    return pltpu.CompilerParams(dimension_semantics=sem, vmem_limit_bytes=VMEM_LIMIT)


def _const_spec(shape):
    zeros = (0,) * len(shape)
    return pl.BlockSpec(shape, lambda *_: zeros, pipeline_mode=pl.Buffered(1))


def _in_proj_kernel(x_ref, g_ref, w_ref, cos_ref, sin_ref,
                    qat_ref, ka_ref, vat_ref, qd_ref, kd_ref, vd_ref):
    h = _rms(x_ref[...], g_ref[...]).astype(_bf16)
    cos = cos_ref[...]
    sin = sin_ref[...]
    lane = lax.broadcasted_iota(jnp.int32, cos.shape, 1)
    low_half = (lane % HEAD_DIM) < (HEAD_DIM // 2)

    def proj(j):
        return _dot(h, w_ref[:, j * DIFF_WIDTH:(j + 1) * DIFF_WIDTH])

    def rope(r, scale):
        outs = []
        for c in range(DIFF_WIDTH // LANES):
            y = r[:, c * LANES:(c + 1) * LANES]
            partner = jnp.where(low_half,
                                pltpu.roll(y, LANES - HEAD_DIM // 2, axis=1),
                                pltpu.roll(y, HEAD_DIM // 2, axis=1))
            y = y * cos + partner * sin
            if scale is not None:
                y = y * scale
            outs.append(y)
        return jnp.concatenate(outs, axis=1)

    q_scale = HEAD_DIM ** -0.5 * LOG2E
    qat_ref[0, 0] = rope(proj(0), q_scale).T.astype(_bf16)
    ka_ref[...] = rope(proj(1), None).astype(_bf16)
    vat_ref[0, 0] = proj(2).T.astype(_bf16)
    qd_ref[...] = rope(proj(3), q_scale).astype(_bf16)
    kd_ref[...] = rope(proj(4), None).astype(_bf16)
    vd_ref[...] = proj(5).astype(_bf16)


def _in_proj(x2d, g, w_in, cos, sin, batch, seq):
    rows, d = x2d.shape
    tm = ROW_BLOCK
    nblk = rows // tm
    per_seq = seq // tm
    row_spec = pl.BlockSpec((tm, DIFF_WIDTH), lambda i: (i, 0))
    t_spec = pl.BlockSpec((1, 1, DIFF_WIDTH, tm), lambda i: (i // per_seq, i % per_seq, 0, 0))
    tab_spec = pl.BlockSpec((tm, LANES), lambda i: (i % per_seq, 0))
    nat = jax.ShapeDtypeStruct((rows, DIFF_WIDTH), _bf16)
    tr = jax.ShapeDtypeStruct((batch, per_seq, DIFF_WIDTH, tm), _bf16)
    return pl.pallas_call(
        _in_proj_kernel,
        grid=(nblk,),
        in_specs=[pl.BlockSpec((tm, d), lambda i: (i, 0)),
                  _const_spec((1, d)),
                  _const_spec(w_in.shape),
                  tab_spec, tab_spec],
        out_specs=[t_spec, row_spec, t_spec, row_spec, row_spec, row_spec],
        out_shape=[tr, nat, tr, nat, nat, nat],
        compiler_params=_params("parallel"),
        name="in_proj",
    )(x2d, g, w_in, cos, sin)


def _diff_attn_kernel(lambda_init, n_kv, qt_ref, k_ref, vt_ref, lam_ref, subg_ref, o_ref,
                      m_ref, l_ref, acc_ref):
    qt = qt_ref[0, 0]
    row = lax.broadcasted_iota(jnp.int32, qt.shape, 0)
    zero = jnp.zeros_like(qt)
    q_comp = (jnp.where(row < HEAD_DIM, qt, zero), jnp.where(row >= HEAD_DIM, qt, zero))

    m_ref[...] = jnp.full(m_ref.shape, NEG_BIG, _f32)
    l_ref[...] = jnp.zeros(l_ref.shape, _f32)
    acc_ref[...] = jnp.zeros(acc_ref.shape, _f32)

    def step(c, carry):
        k = k_ref[0, c]
        vt = vt_ref[0, c]
        for comp in range(2):
            s = _dot(k, q_comp[comp])
            m_old = m_ref[comp]
            m_new = jnp.maximum(m_old, jnp.max(s, axis=0, keepdims=True))
            alpha = jnp.exp2(m_old - m_new)
            p = jnp.exp2(s - m_new)
            l_ref[comp] = alpha * l_ref[comp] + jnp.sum(p, axis=0, keepdims=True)
            acc_ref[comp] = alpha * acc_ref[comp] + _dot(vt, p.astype(_bf16))
            m_ref[comp] = m_new
        return carry

    lax.fori_loop(0, n_kv, step, 0)

    lam_v = lam_ref[...]
    lam = (jnp.exp(jnp.sum(lam_v[0:1] * lam_v[1:2], axis=-1, keepdims=True))
           - jnp.exp(jnp.sum(lam_v[2:3] * lam_v[3:4], axis=-1, keepdims=True))
           + lambda_init)
    o = acc_ref[0] / l_ref[0] - lam * (acc_ref[1] / l_ref[1])
    o = o.T
    o = _rms(o, subg_ref[...]) * (1.0 - lambda_init)
    o_ref[...] = o.astype(_bf16)


def _diff_attn(qat, ka, vat, lam_vecs, sub_g, lambda_init, batch, seq):
    tm = ROW_BLOCK
    per_seq = seq // tm
    hw = 2 * HEAD_DIM
    k4 = ka.reshape(batch, per_seq, tm, DIFF_WIDTH)
    kernel = functools.partial(_diff_attn_kernel, lambda_init, per_seq)
    return pl.pallas_call(
        kernel,
        grid=(batch, N_DIFF_HEADS, per_seq),
        in_specs=[pl.BlockSpec((1, 1, hw, tm), lambda b, h, i: (b, i, h, 0)),
                  pl.BlockSpec((1, per_seq, tm, hw), lambda b, h, i: (b, 0, 0, h)),
                  pl.BlockSpec((1, per_seq, hw, tm), lambda b, h, i: (b, 0, h, 0)),
                  pl.BlockSpec((4, HEAD_DIM), lambda b, h, i: (0, 0)),
                  pl.BlockSpec((1, hw), lambda b, h, i: (0, 0))],
        out_specs=pl.BlockSpec((tm, hw), lambda b, h, i: (b * per_seq + i, h)),
        out_shape=jax.ShapeDtypeStruct((batch * seq, DIFF_WIDTH), _bf16),
        scratch_shapes=[pltpu.VMEM((2, 1, tm), _f32),
                        pltpu.VMEM((2, 1, tm), _f32),
                        pltpu.VMEM((2, hw, tm), _f32)],
        compiler_params=_params("parallel", "parallel", "arbitrary"),
        name="diff_attn",
    )(qat, k4, vat, lam_vecs, sub_g)


def _dil_attn_kernel(stream_len, tqb,
                     q_ref, kp_ref, km_ref, kn_ref, vp_ref, vm_ref, vn_ref, o_ref, lse_ref):
    j = pl.program_id(2)
    k_ext = jnp.concatenate([kp_ref[0], km_ref[0], kn_ref[0]], axis=0)
    v_ext = jnp.concatenate([vp_ref[0], vm_ref[0], vn_ref[0]], axis=0)
    span = DIL_Q + 2 * DIL_HALF
    qi = lax.broadcasted_iota(jnp.int32, (DIL_Q, span), 0)
    kj = lax.broadcasted_iota(jnp.int32, (DIL_Q, span), 1)
    lane = lax.broadcasted_iota(jnp.int32, (DIL_Q, LANES), 1)
    first_head = lane < HEAD_DIM
    for t in range(tqb // DIL_Q):
        base = j * tqb + t * DIL_Q - DIL_HALF
        kpos = kj + base
        ok = jnp.where(kj >= qi, 1, 0) * jnp.where(kj <= qi + 2 * DIL_HALF, 1, 0)
        ok = ok * jnp.where(kpos >= 0, 1, 0) * jnp.where(kpos < stream_len, 1, 0)
        ok = ok > 0
        q = q_ref[0, t * DIL_Q:(t + 1) * DIL_Q, :]
        k = k_ext[t * DIL_Q:t * DIL_Q + span]
        v = v_ext[t * DIL_Q:t * DIL_Q + span]
        for c in range(DIL_WIDTH // LANES):
            qc = q[:, c * LANES:(c + 1) * LANES]
            kc = k[:, c * LANES:(c + 1) * LANES]
            vc = v[:, c * LANES:(c + 1) * LANES]
            halves = []
            for hh in range(2):
                keep = first_head if hh == 0 else jnp.logical_not(first_head)
                qm = jnp.where(keep, qc, jnp.zeros_like(qc))
                s = jnp.where(ok, _dot_nt(qm, kc), NEG_BIG)
                m = jnp.max(s, axis=-1, keepdims=True)
                p = jnp.exp2(s - m)
                l = jnp.sum(p, axis=-1, keepdims=True)
                pv = _dot(p.astype(_bf16), vc)
                halves.append((pv / l, m + jnp.log2(l)))
            o = jnp.where(first_head, halves[0][0], halves[1][0])
            lse = jnp.where(first_head, halves[0][1], halves[1][1])
            o_ref[0, t * DIL_Q:(t + 1) * DIL_Q, c * LANES:(c + 1) * LANES] = o.astype(_bf16)
            lse_ref[0, t * DIL_Q:(t + 1) * DIL_Q, c * LANES:(c + 1) * LANES] = lse


def _dil_attn(qd, kd, vd, dilation, batch, seq):
    r = dilation
    sr = seq // r
    tqb = min(2 * DIL_Q, sr)
    nblk = sr // tqb
    nh = sr // DIL_HALF
    per = tqb // DIL_HALF
    view = lambda a: a.reshape(batch, sr, r * DIL_WIDTH)
    main = pl.BlockSpec((1, tqb, DIL_WIDTH), lambda b, i, j: (b, j, i))
    prev = pl.BlockSpec((1, DIL_HALF, DIL_WIDTH),
                        lambda b, i, j: (b, jnp.maximum(j * per - 1, 0), i))
    nxt = pl.BlockSpec((1, DIL_HALF, DIL_WIDTH),
                       lambda b, i, j: (b, jnp.minimum((j + 1) * per, nh - 1), i))
    q3, k3, v3 = view(qd), view(kd), view(vd)
    o, lse = pl.pallas_call(
        functools.partial(_dil_attn_kernel, sr, tqb),
        grid=(batch, r, nblk),
        in_specs=[main, prev, main, nxt, prev, main, nxt],
        out_specs=[main, main],
        out_shape=[jax.ShapeDtypeStruct((batch, sr, r * DIL_WIDTH), _bf16),
                   jax.ShapeDtypeStruct((batch, sr, r * DIL_WIDTH), _f32)],
        compiler_params=_params("parallel", "parallel", "parallel"),
        name="dil_attn_r%d" % r,
    )(q3, k3, k3, k3, v3, v3, v3)
    return o.reshape(batch * seq, DIL_WIDTH), lse.reshape(batch * seq, DIL_WIDTH)


def _mem_kv_kernel(mem_ref, g_ref, w_ref, kv_ref):
    h = _rms(mem_ref[0], g_ref[...]).astype(_bf16)
    kv_ref[0] = _dot(h, w_ref[...]).astype(_bf16)


def _mem_kv(mem, g, w_xkv):
    batch, n_mem, d = mem.shape
    return pl.pallas_call(
        _mem_kv_kernel,
        grid=(batch,),
        in_specs=[pl.BlockSpec((1, n_mem, d), lambda b: (b, 0, 0)),
                  _const_spec((1, d)),
                  _const_spec(w_xkv.shape)],
        out_specs=pl.BlockSpec((1, n_mem, 2 * d), lambda b: (b, 0, 0)),
        out_shape=jax.ShapeDtypeStruct((batch, n_mem, 2 * d), _bf16),
        compiler_params=_params("parallel"),
        name="mem_kv",
    )(mem, g, w_xkv)


def _mid_kernel(x_ref, oa_ref, o1_ref, o2_ref, o3_ref, l1_ref, l2_ref, l3_ref, kv_ref,
                wout_ref, wq_ref, wo_ref, gmix_ref, gpre_ref, gpost_ref, y_ref):
    lses = (l1_ref[...], l2_ref[...], l3_ref[...])
    outs = (o1_ref[...], o2_ref[...], o3_ref[...])
    m = jnp.maximum(jnp.maximum(lses[0], lses[1]), lses[2])
    es = [jnp.exp2(l - m) for l in lses]
    den = es[0] + es[1] + es[2]
    od = (es[0] * outs[0].astype(_f32) + es[1] * outs[1].astype(_f32)
          + es[2] * outs[2].astype(_f32)) / den
    mixed = jnp.concatenate([oa_ref[...], od.astype(_bf16)], axis=1)
    x1 = x_ref[...] + _rms(_dot(mixed, wout_ref[...]), gmix_ref[...])

    d = x1.shape[1]
    dh = d // N_CROSS_HEADS
    h = _rms(x1, gpre_ref[...]).astype(_bf16)
    q = (_dot(h, wq_ref[...]) * (dh ** -0.5 * LOG2E)).astype(_bf16)
    kv = kv_ref[0]
    heads = []
    for hd in range(N_CROSS_HEADS):
        s = _dot_nt(q[:, hd * dh:(hd + 1) * dh], kv[:, hd * dh:(hd + 1) * dh])
        p = jnp.exp2(s - jnp.max(s, axis=-1, keepdims=True))
        l = jnp.sum(p, axis=-1, keepdims=True)
        o = _dot(p.astype(_bf16), kv[:, d + hd * dh:d + (hd + 1) * dh]) / l
        heads.append(o.astype(_bf16))
    c = _dot(jnp.concatenate(heads, axis=1), wo_ref[...])
    y_ref[...] = x1 + _rms(c, gpost_ref[...])


def _mid(x2d, oa, dil, kv, w_out, w_xq, w_xo, g_mix, g_pre, g_post, seq):
    rows, d = x2d.shape
    tm = ROW_BLOCK
    per_seq = seq // tm
    n_mem = kv.shape[1]
    half = pl.BlockSpec((tm, DIL_WIDTH), lambda i: (i, 0))
    full = pl.BlockSpec((tm, d), lambda i: (i, 0))
    (o1, l1), (o2, l2), (o3, l3) = dil
    return pl.pallas_call(
        _mid_kernel,
        grid=(rows // tm,),
        in_specs=[full, half, half, half, half, half, half, half,
                  pl.BlockSpec((1, n_mem, 2 * d), lambda i: (i // per_seq, 0, 0)),
                  _const_spec(w_out.shape), _const_spec(w_xq.shape), _const_spec(w_xo.shape),
                  _const_spec((1, d)), _const_spec((1, d)), _const_spec((1, d))],
        out_specs=full,
        out_shape=jax.ShapeDtypeStruct((rows, d), _f32),
        compiler_params=_params("parallel"),
        name="mid",
    )(x2d, oa, o1, o2, o3, l1, l2, l3, kv, w_out, w_xq, w_xo, g_mix, g_pre, g_post)


def _gelu_tanh(x):
    return 0.5 * x * (1.0 + jnp.tanh(math.sqrt(2.0 / math.pi) * (x + 0.044715 * (x * x * x))))


def _ffn_kernel(per_seq, d_ff, x_ref, xp_ref, xn_ref, gpre_ref, wup_ref, cw_ref, cb_ref,
                wdn_ref, gpost_ref, y_ref):
    i = pl.program_id(0)
    tm = x_ref.shape[0]
    g = gpre_ref[...]
    x = x_ref[...]
    has_prev = jnp.where(i % per_seq != 0, 1.0, 0.0)
    has_next = jnp.where(i % per_seq != per_seq - 1, 1.0, 0.0)
    h_ext = jnp.concatenate([_rms(xp_ref[...], g) * has_prev,
                             _rms(x, g),
                             _rms(xn_ref[...], g) * has_next], axis=0).astype(_bf16)
    ext = tm + 2 * CONV_HALO
    acc = jnp.zeros(x.shape, _f32)
    for c in range(d_ff // FFN_CHUNK):
        parts = []
        for off in (0, d_ff):
            lo = off + c * FFN_CHUNK
            u = _dot(h_ext, wup_ref[:, lo:lo + FFN_CHUNK])
            u_prev = pltpu.roll(u, 1, axis=0)[CONV_HALO:CONV_HALO + tm]
            u_next = pltpu.roll(u, ext - 1, axis=0)[CONV_HALO:CONV_HALO + tm]
            u_mid = u[CONV_HALO:CONV_HALO + tm]
            cw = cw_ref[:, lo:lo + FFN_CHUNK]
            parts.append(cb_ref[:, lo:lo + FFN_CHUNK] + u_prev * cw[0:1] + u_mid * cw[1:2]
                         + u_next * cw[2:3])
        act = (_gelu_tanh(parts[0]) * parts[1]).astype(_bf16)
        acc = acc + _dot(act, wdn_ref[c * FFN_CHUNK:(c + 1) * FFN_CHUNK, :])
    y_ref[...] = x + _rms(acc, gpost_ref[...])


def _ffn(x2d, g_pre, w_up, conv_w, conv_b, w_down, g_post, seq):
    rows, d = x2d.shape
    tm = ROW_BLOCK
    per_seq = seq // tm
    d_ff = w_down.shape[0]
    per = tm // CONV_HALO
    last = rows // CONV_HALO - 1
    return pl.pallas_call(
        functools.partial(_ffn_kernel, per_seq, d_ff),
        grid=(rows // tm,),
        in_specs=[pl.BlockSpec((tm, d), lambda i: (i, 0)),
                  pl.BlockSpec((CONV_HALO, d), lambda i: (jnp.maximum(i * per - 1, 0), 0)),
                  pl.BlockSpec((CONV_HALO, d), lambda i: (jnp.minimum((i + 1) * per, last), 0)),
                  _const_spec((1, d)),
                  _const_spec(w_up.shape),
                  _const_spec(conv_w.shape),
                  _const_spec(conv_b.shape),
                  _const_spec(w_down.shape),
                  _const_spec((1, d))],
        out_specs=pl.BlockSpec((tm, d), lambda i: (i, 0)),
        out_shape=jax.ShapeDtypeStruct((rows, d), _f32),
        compiler_params=_params("parallel"),
        name="ffn",
    )(x2d, x2d, x2d, g_pre, w_up, conv_w, conv_b, w_down, g_post)


def _rope_tables(seq):
    inv = 1.0 / (ROPE_THETA ** (jnp.arange(0, HEAD_DIM, 2, dtype=_f32) / HEAD_DIM))
    ang = jnp.arange(seq, dtype=_f32)[:, None] * inv[None, :]
    cos, sin = jnp.cos(ang), jnp.sin(ang)
    reps = LANES // HEAD_DIM
    cos_t = jnp.tile(jnp.concatenate([cos, cos], axis=1), (1, reps))
    sin_t = jnp.tile(jnp.concatenate([-sin, sin], axis=1), (1, reps))
    return cos_t, sin_t


def _trunk(x, mem, p, depth):
    batch, seq, d = x.shape
    cos, sin = _rope_tables(seq)
    x2d = x.reshape(batch * seq, d)
    row = lambda v: v.reshape(1, -1)
    for l in range(depth):
        lambda_init = 0.8 - 0.6 * math.exp(-0.3 * l)
        qat, ka, vat, qd, kd, vd = _in_proj(x2d, row(p['mix_pre_g'][l]), p['w_in'][l],
                                            cos, sin, batch, seq)
        lam_vecs = jnp.stack([p['lambda_q1'][l], p['lambda_k1'][l],
                              p['lambda_q2'][l], p['lambda_k2'][l]])
        oa = _diff_attn(qat, ka, vat, lam_vecs, row(p['diff_subln_g'][l]), lambda_init,
                        batch, seq)
        dil = [_dil_attn(qd, kd, vd, r, batch, seq) for (_, r) in DIL_PATTERNS]
        kv = _mem_kv(mem, row(p['mem_norm_g'][l]), p['w_xkv'][l])
        x2d = _mid(x2d, oa, dil, kv, p['w_out'][l], p['w_xq'][l], p['w_xo'][l],
                   row(p['mix_post_g'][l]), row(p['xattn_pre_g'][l]),
                   row(p['xattn_post_g'][l]), seq)
        x2d = _ffn(x2d, row(p['ffn_pre_g'][l]), p['w_up'][l], p['conv_w'][l],
                   row(p['conv_b'][l]), p['w_down'][l], row(p['ffn_post_g'][l]), seq)
    return x2d.reshape(batch, seq, d)


def kernel(x_prompt, x_sample, mem_prompt, mem_sample, w_in, w_out, lambda_q1, lambda_k1, lambda_q2, lambda_k2, diff_subln_g, w_xq, w_xkv, w_xo, w_up, conv_w, conv_b, w_down, mix_pre_g, mix_post_g, mem_norm_g, xattn_pre_g, xattn_post_g, ffn_pre_g, ffn_post_g):
    depth = w_in.shape[0]
    p = {
        'w_in': w_in.astype(_bf16), 'w_out': w_out.astype(_bf16),
        'lambda_q1': lambda_q1, 'lambda_k1': lambda_k1,
        'lambda_q2': lambda_q2, 'lambda_k2': lambda_k2,
        'diff_subln_g': diff_subln_g,
        'w_xq': w_xq.astype(_bf16), 'w_xkv': w_xkv.astype(_bf16), 'w_xo': w_xo.astype(_bf16),
        'w_up': w_up.astype(_bf16), 'conv_w': conv_w, 'conv_b': conv_b,
        'w_down': w_down.astype(_bf16),
        'mix_pre_g': mix_pre_g, 'mix_post_g': mix_post_g, 'mem_norm_g': mem_norm_g,
        'xattn_pre_g': xattn_pre_g, 'xattn_post_g': xattn_post_g,
        'ffn_pre_g': ffn_pre_g, 'ffn_post_g': ffn_post_g,
    }
    y_prompt = _trunk(x_prompt, mem_prompt, p, depth)
    y_sample = _trunk(x_sample, mem_sample, p, depth)
    return (y_prompt, y_sample)
```

```python
import functools
import math

import jax
import jax.numpy as jnp
from jax import lax
from jax.experimental import pallas as pl
from jax.experimental.pallas import tpu as pltpu

HEAD_DIM = 64
N_DIFF_HEADS = 4
DIFF_WIDTH = N_DIFF_HEADS * 2 * HEAD_DIM
N_DIL_HEADS = 8
DIL_WIDTH = N_DIL_HEADS * HEAD_DIM
DIL_PATTERNS = ((128, 1), (512, 4), (2048, 16))
DIL_HALF = 64
ROPE_THETA = 10000.0
N_CROSS_HEADS = 4
CONV_WIDTH = 3
RMS_EPS = 1e-6
LOG2E = 1.4426950408889634

LANES = 128
ROW_BLOCK = 512
DIFF_TQ = 256
DIFF_TK = 1024
DIL_Q = 128
CONV_HALO = 8
FFN_CHUNK = 256
NEG_BIG = -1e30
VMEM_LIMIT = 56 * 1024 * 1024

_f32 = jnp.float32
_bf16 = jnp.bfloat16


def _rms(x, g):
    return x * lax.rsqrt(jnp.mean(x * x, axis=-1, keepdims=True) + RMS_EPS) * g


def _dot(a, b):
    return jnp.dot(a, b, preferred_element_type=_f32)


def _dot_nt(a, b):
    return lax.dot_general(a, b, (((1,), (1,)), ((), ())), preferred_element_type=_f32)


def _params(*sem):
    return pltpu.CompilerParams(dimension_semantics=sem, vmem_limit_bytes=VMEM_LIMIT)


def _const_spec(shape):
    zeros = (0,) * len(shape)
    return pl.BlockSpec(shape, lambda *_: zeros, pipeline_mode=pl.Buffered(1))


def _in_proj_kernel(x_ref, g_ref, w_ref, cos_ref, sin_ref,
                    qat_ref, ka_ref, vat_ref, qd_ref, kd_ref, vd_ref):
    h = _rms(x_ref[...], g_ref[...]).astype(_bf16)
    cos = cos_ref[...]
    sin = sin_ref[...]
    lane = lax.broadcasted_iota(jnp.int32, cos.shape, 1)
    low_half = (lane % HEAD_DIM) < (HEAD_DIM // 2)

    def proj(j):
        return _dot(h, w_ref[:, j * DIFF_WIDTH:(j + 1) * DIFF_WIDTH])

    def rope(r, scale):
        outs = []
        for c in range(DIFF_WIDTH // LANES):
            y = r[:, c * LANES:(c + 1) * LANES]
            partner = jnp.where(low_half,
                                pltpu.roll(y, LANES - HEAD_DIM // 2, axis=1),
                                pltpu.roll(y, HEAD_DIM // 2, axis=1))
            y = y * cos + partner * sin
            if scale is not None:
                y = y * scale
            outs.append(y)
        return jnp.concatenate(outs, axis=1)

    q_scale = HEAD_DIM ** -0.5 * LOG2E
    qat_ref[0, 0] = rope(proj(0), q_scale).T.astype(_bf16)
    ka_ref[...] = rope(proj(1), None).astype(_bf16)
    vat_ref[0, 0] = proj(2).T.astype(_bf16)
    qd_ref[...] = rope(proj(3), q_scale).astype(_bf16)
    kd_ref[...] = rope(proj(4), None).astype(_bf16)
    vd_ref[...] = proj(5).astype(_bf16)


def _in_proj(x2d, g, w_in, cos, sin, batch, seq):
    rows, d = x2d.shape
    tm = ROW_BLOCK
    nblk = rows // tm
    per_seq = seq // tm
    row_spec = pl.BlockSpec((tm, DIFF_WIDTH), lambda i: (i, 0))
    t_spec = pl.BlockSpec((1, 1, DIFF_WIDTH, tm), lambda i: (i // per_seq, i % per_seq, 0, 0))
    tab_spec = pl.BlockSpec((tm, LANES), lambda i: (i % per_seq, 0))
    nat = jax.ShapeDtypeStruct((rows, DIFF_WIDTH), _bf16)
    tr = jax.ShapeDtypeStruct((batch, per_seq, DIFF_WIDTH, tm), _bf16)
    return pl.pallas_call(
        _in_proj_kernel,
        grid=(nblk,),
        in_specs=[pl.BlockSpec((tm, d), lambda i: (i, 0)),
                  _const_spec((1, d)),
                  _const_spec(w_in.shape),
                  tab_spec, tab_spec],
        out_specs=[t_spec, row_spec, t_spec, row_spec, row_spec, row_spec],
        out_shape=[tr, nat, tr, nat, nat, nat],
        compiler_params=_params("parallel"),
        name="in_proj",
    )(x2d, g, w_in, cos, sin)


def _diff_attn_kernel(lambda_init, n_kv, qt_ref, k_ref, vt_ref, lam_ref, subg_ref, o_ref,
                      s0, s1, bm0, bm1, p0, p1, a0, a1, m_ref, l_ref, acc_ref):
    qt = qt_ref[0, 0]
    row = lax.broadcasted_iota(jnp.int32, qt.shape, 0)
    zero = jnp.zeros_like(qt)
    q_both = jnp.concatenate([jnp.where(row < HEAD_DIM, qt, zero),
                              jnp.where(row >= HEAD_DIM, qt, zero)], axis=1)
    sub = vt_ref.shape[3]
    n_sub = s0.shape[0] // sub

    def scores(c, s_ref, bm_ref):
        s = _dot(k_ref[0, c], q_both)
        s_ref[...] = s
        bm_ref[...] = jnp.max(s, axis=0, keepdims=True)

    def softmax(s_ref, bm_ref, p_ref, a_ref):
        m_old = m_ref[...]
        m_new = jnp.maximum(m_old, bm_ref[...])
        alpha = jnp.exp2(m_old - m_new)
        p = jnp.exp2(s_ref[...] - m_new)
        l_ref[...] = alpha * l_ref[...] + jnp.sum(p, axis=0, keepdims=True)
        p_ref[...] = p.astype(_bf16)
        a_ref[...] = alpha
        m_ref[...] = m_new

    def pv(c, p_ref, a_ref):
        upd = a_ref[...] * acc_ref[...]
        for u in range(n_sub):
            upd = upd + _dot(vt_ref[0, c * n_sub + u], p_ref[u * sub:(u + 1) * sub, :])
        acc_ref[...] = upd

    m_ref[...] = jnp.full(m_ref.shape, NEG_BIG, _f32)
    l_ref[...] = jnp.zeros(l_ref.shape, _f32)
    acc_ref[...] = jnp.zeros(acc_ref.shape, _f32)

    scores(0, s0, bm0)
    scores(1, s1, bm1)
    softmax(s0, bm0, p0, a0)
    scores(2, s0, bm0)
    softmax(s1, bm1, p1, a1)
    pv(0, p0, a0)

    def pair(j, carry):
        c = 2 * j
        scores(c + 1, s1, bm1)
        softmax(s0, bm0, p0, a0)
        pv(c - 1, p1, a1)
        scores(c + 2, s0, bm0)
        softmax(s1, bm1, p1, a1)
        pv(c, p0, a0)
        return carry

    lax.fori_loop(1, n_kv // 2 - 1, pair, 0)

    scores(n_kv - 1, s1, bm1)
    softmax(s0, bm0, p0, a0)
    pv(n_kv - 3, p1, a1)
    softmax(s1, bm1, p1, a1)
    pv(n_kv - 2, p0, a0)
    pv(n_kv - 1, p1, a1)

    lam_v = lam_ref[...]
    lam = (jnp.exp(jnp.sum(lam_v[0:1] * lam_v[1:2], axis=-1, keepdims=True))
           - jnp.exp(jnp.sum(lam_v[2:3] * lam_v[3:4], axis=-1, keepdims=True))
           + lambda_init)
    tq = qt.shape[1]
    o = acc_ref[...] / l_ref[...]
    o = (o[:, :tq] - lam * o[:, tq:]).T
    o = _rms(o, subg_ref[...]) * (1.0 - lambda_init)
    o_ref[...] = o.astype(_bf16)


def _diff_attn(qat, ka, vat, lam_vecs, sub_g, lambda_init, batch, seq):
    tm = ROW_BLOCK
    tq, tk = DIFF_TQ, DIFF_TK
    n_kv = seq // tk
    assert n_kv >= 4 and n_kv % 2 == 0, "the unrolled pipeline needs an even number (>= 4) of key chunks"
    hw = 2 * HEAD_DIM
    k4 = ka.reshape(batch, n_kv, tk, DIFF_WIDTH)
    kernel = functools.partial(_diff_attn_kernel, lambda_init, n_kv)
    wide = lambda rows, dtype: pltpu.VMEM((rows, 2 * tq), dtype)
    return pl.pallas_call(
        kernel,
        grid=(batch, N_DIFF_HEADS, seq // tq),
        in_specs=[pl.BlockSpec((1, 1, hw, tq),
                               lambda b, h, i: (b, i // (tm // tq), h, i % (tm // tq))),
                  pl.BlockSpec((1, n_kv, tk, hw), lambda b, h, i: (b, 0, 0, h)),
                  pl.BlockSpec((1, seq // tm, hw, tm), lambda b, h, i: (b, 0, h, 0)),
                  pl.BlockSpec((4, HEAD_DIM), lambda b, h, i: (0, 0)),
                  pl.BlockSpec((1, hw), lambda b, h, i: (0, 0))],
        out_specs=pl.BlockSpec((tq, hw), lambda b, h, i: (b * (seq // tq) + i, h)),
        out_shape=jax.ShapeDtypeStruct((batch * seq, DIFF_WIDTH), _bf16),
        scratch_shapes=[wide(tk, _f32), wide(tk, _f32),
                        wide(1, _f32), wide(1, _f32),
                        wide(tk, _bf16), wide(tk, _bf16),
                        wide(1, _f32), wide(1, _f32),
                        wide(1, _f32), wide(1, _f32),
                        wide(hw, _f32)],
        compiler_params=_params("parallel", "parallel", "arbitrary"),
        name="diff_attn",
    )(qat, k4, vat, lam_vecs, sub_g)


def _dil_attn_kernel(stream_len, tqb,
                     q_ref, kp_ref, km_ref, kn_ref, vp_ref, vm_ref, vn_ref, o_ref, lse_ref):
    j = pl.program_id(2)
    k_ext = jnp.concatenate([kp_ref[0], km_ref[0], kn_ref[0]], axis=0)
    v_ext = jnp.concatenate([vp_ref[0], vm_ref[0], vn_ref[0]], axis=0)
    span = DIL_Q + 2 * DIL_HALF
    qi = lax.broadcasted_iota(jnp.int32, (DIL_Q, span), 0)
    kj = lax.broadcasted_iota(jnp.int32, (DIL_Q, span), 1)
    lane = lax.broadcasted_iota(jnp.int32, (DIL_Q, LANES), 1)
    first_head = lane < HEAD_DIM
    for t in range(tqb // DIL_Q):
        base = j * tqb + t * DIL_Q - DIL_HALF
        kpos = kj + base
        ok = jnp.where(kj >= qi, 1, 0) * jnp.where(kj <= qi + 2 * DIL_HALF, 1, 0)
        ok = ok * jnp.where(kpos >= 0, 1, 0) * jnp.where(kpos < stream_len, 1, 0)
        ok = ok > 0
        q = q_ref[0, t * DIL_Q:(t + 1) * DIL_Q, :]
        k = k_ext[t * DIL_Q:t * DIL_Q + span]
        v = v_ext[t * DIL_Q:t * DIL_Q + span]
        for c in range(DIL_WIDTH // LANES):
            qc = q[:, c * LANES:(c + 1) * LANES]
            kc = k[:, c * LANES:(c + 1) * LANES]
            vc = v[:, c * LANES:(c + 1) * LANES]
            halves = []
            for hh in range(2):
                keep = first_head if hh == 0 else jnp.logical_not(first_head)
                qm = jnp.where(keep, qc, jnp.zeros_like(qc))
                s = jnp.where(ok, _dot_nt(qm, kc), NEG_BIG)
                m = jnp.max(s, axis=-1, keepdims=True)
                p = jnp.exp2(s - m)
                l = jnp.sum(p, axis=-1, keepdims=True)
                pv = _dot(p.astype(_bf16), vc)
                halves.append((pv / l, m + jnp.log2(l)))
            o = jnp.where(first_head, halves[0][0], halves[1][0])
            lse = jnp.where(first_head, halves[0][1], halves[1][1])
            o_ref[0, t * DIL_Q:(t + 1) * DIL_Q, c * LANES:(c + 1) * LANES] = o.astype(_bf16)
            lse_ref[0, t * DIL_Q:(t + 1) * DIL_Q, c * LANES:(c + 1) * LANES] = lse


def _dil_attn(qd, kd, vd, dilation, batch, seq):
    r = dilation
    sr = seq // r
    tqb = min(2 * DIL_Q, sr)
    nblk = sr // tqb
    nh = sr // DIL_HALF
    per = tqb // DIL_HALF
    view = lambda a: a.reshape(batch, sr, r * DIL_WIDTH)
    main = pl.BlockSpec((1, tqb, DIL_WIDTH), lambda b, i, j: (b, j, i))
    prev = pl.BlockSpec((1, DIL_HALF, DIL_WIDTH),
                        lambda b, i, j: (b, jnp.maximum(j * per - 1, 0), i))
    nxt = pl.BlockSpec((1, DIL_HALF, DIL_WIDTH),
                       lambda b, i, j: (b, jnp.minimum((j + 1) * per, nh - 1), i))
    q3, k3, v3 = view(qd), view(kd), view(vd)
    o, lse = pl.pallas_call(
        functools.partial(_dil_attn_kernel, sr, tqb),
        grid=(batch, r, nblk),
        in_specs=[main, prev, main, nxt, prev, main, nxt],
        out_specs=[main, main],
        out_shape=[jax.ShapeDtypeStruct((batch, sr, r * DIL_WIDTH), _bf16),
                   jax.ShapeDtypeStruct((batch, sr, r * DIL_WIDTH), _f32)],
        compiler_params=_params("parallel", "parallel", "parallel"),
        name="dil_attn_r%d" % r,
    )(q3, k3, k3, k3, v3, v3, v3)
    return o.reshape(batch * seq, DIL_WIDTH), lse.reshape(batch * seq, DIL_WIDTH)


def _mem_kv_kernel(mem_ref, g_ref, w_ref, kv_ref):
    h = _rms(mem_ref[0], g_ref[...]).astype(_bf16)
    kv_ref[0] = _dot(h, w_ref[...]).astype(_bf16)


def _mem_kv(mem, g, w_xkv):
    batch, n_mem, d = mem.shape
    return pl.pallas_call(
        _mem_kv_kernel,
        grid=(batch,),
        in_specs=[pl.BlockSpec((1, n_mem, d), lambda b: (b, 0, 0)),
                  _const_spec((1, d)),
                  _const_spec(w_xkv.shape)],
        out_specs=pl.BlockSpec((1, n_mem, 2 * d), lambda b: (b, 0, 0)),
        out_shape=jax.ShapeDtypeStruct((batch, n_mem, 2 * d), _bf16),
        compiler_params=_params("parallel"),
        name="mem_kv",
    )(mem, g, w_xkv)


def _mid_kernel(x_ref, oa_ref, o1_ref, o2_ref, o3_ref, l1_ref, l2_ref, l3_ref, kv_ref,
                wout_ref, wq_ref, wo_ref, gmix_ref, gpre_ref, gpost_ref, y_ref):
    lses = (l1_ref[...], l2_ref[...], l3_ref[...])
    outs = (o1_ref[...], o2_ref[...], o3_ref[...])
    m = jnp.maximum(jnp.maximum(lses[0], lses[1]), lses[2])
    es = [jnp.exp2(l - m) for l in lses]
    den = es[0] + es[1] + es[2]
    od = (es[0] * outs[0].astype(_f32) + es[1] * outs[1].astype(_f32)
          + es[2] * outs[2].astype(_f32)) / den
    mixed = jnp.concatenate([oa_ref[...], od.astype(_bf16)], axis=1)
    x1 = x_ref[...] + _rms(_dot(mixed, wout_ref[...]), gmix_ref[...])

    d = x1.shape[1]
    dh = d // N_CROSS_HEADS
    h = _rms(x1, gpre_ref[...]).astype(_bf16)
    q = (_dot(h, wq_ref[...]) * (dh ** -0.5 * LOG2E)).astype(_bf16)
    kv = kv_ref[0]
    heads = []
    for hd in range(N_CROSS_HEADS):
        s = _dot_nt(q[:, hd * dh:(hd + 1) * dh], kv[:, hd * dh:(hd + 1) * dh])
        p = jnp.exp2(s - jnp.max(s, axis=-1, keepdims=True))
        l = jnp.sum(p, axis=-1, keepdims=True)
        o = _dot(p.astype(_bf16), kv[:, d + hd * dh:d + (hd + 1) * dh]) / l
        heads.append(o.astype(_bf16))
    c = _dot(jnp.concatenate(heads, axis=1), wo_ref[...])
    y_ref[...] = x1 + _rms(c, gpost_ref[...])


def _mid(x2d, oa, dil, kv, w_out, w_xq, w_xo, g_mix, g_pre, g_post, seq):
    rows, d = x2d.shape
    tm = ROW_BLOCK
    per_seq = seq // tm
    n_mem = kv.shape[1]
    half = pl.BlockSpec((tm, DIL_WIDTH), lambda i: (i, 0))
    full = pl.BlockSpec((tm, d), lambda i: (i, 0))
    (o1, l1), (o2, l2), (o3, l3) = dil
    return pl.pallas_call(
        _mid_kernel,
        grid=(rows // tm,),
        in_specs=[full, half, half, half, half, half, half, half,
                  pl.BlockSpec((1, n_mem, 2 * d), lambda i: (i // per_seq, 0, 0)),
                  _const_spec(w_out.shape), _const_spec(w_xq.shape), _const_spec(w_xo.shape),
                  _const_spec((1, d)), _const_spec((1, d)), _const_spec((1, d))],
        out_specs=full,
        out_shape=jax.ShapeDtypeStruct((rows, d), _f32),
        compiler_params=_params("parallel"),
        name="mid",
    )(x2d, oa, o1, o2, o3, l1, l2, l3, kv, w_out, w_xq, w_xo, g_mix, g_pre, g_post)


def _gelu_tanh(x):
    return 0.5 * x * (1.0 + jnp.tanh(math.sqrt(2.0 / math.pi) * (x + 0.044715 * (x * x * x))))


def _ffn_kernel(per_seq, d_ff, x_ref, xp_ref, xn_ref, gpre_ref, wup_ref, cw_ref, cb_ref,
                wdn_ref, gpost_ref, y_ref):
    i = pl.program_id(0)
    tm = x_ref.shape[0]
    g = gpre_ref[...]
    x = x_ref[...]
    has_prev = jnp.where(i % per_seq != 0, 1.0, 0.0)
    has_next = jnp.where(i % per_seq != per_seq - 1, 1.0, 0.0)
    h_ext = jnp.concatenate([_rms(xp_ref[...], g) * has_prev,
                             _rms(x, g),
                             _rms(xn_ref[...], g) * has_next], axis=0).astype(_bf16)
    ext = tm + 2 * CONV_HALO
    acc = jnp.zeros(x.shape, _f32)
    for c in range(d_ff // FFN_CHUNK):
        parts = []
        for off in (0, d_ff):
            lo = off + c * FFN_CHUNK
            u = _dot(h_ext, wup_ref[:, lo:lo + FFN_CHUNK])
            u_prev = pltpu.roll(u, 1, axis=0)[CONV_HALO:CONV_HALO + tm]
            u_next = pltpu.roll(u, ext - 1, axis=0)[CONV_HALO:CONV_HALO + tm]
            u_mid = u[CONV_HALO:CONV_HALO + tm]
            cw = cw_ref[:, lo:lo + FFN_CHUNK]
            parts.append(cb_ref[:, lo:lo + FFN_CHUNK] + u_prev * cw[0:1] + u_mid * cw[1:2]
                         + u_next * cw[2:3])
        act = (_gelu_tanh(parts[0]) * parts[1]).astype(_bf16)
        acc = acc + _dot(act, wdn_ref[c * FFN_CHUNK:(c + 1) * FFN_CHUNK, :])
    y_ref[...] = x + _rms(acc, gpost_ref[...])


def _ffn(x2d, g_pre, w_up, conv_w, conv_b, w_down, g_post, seq):
    rows, d = x2d.shape
    tm = ROW_BLOCK
    per_seq = seq // tm
    d_ff = w_down.shape[0]
    per = tm // CONV_HALO
    last = rows // CONV_HALO - 1
    return pl.pallas_call(
        functools.partial(_ffn_kernel, per_seq, d_ff),
        grid=(rows // tm,),
        in_specs=[pl.BlockSpec((tm, d), lambda i: (i, 0)),
                  pl.BlockSpec((CONV_HALO, d), lambda i: (jnp.maximum(i * per - 1, 0), 0)),
                  pl.BlockSpec((CONV_HALO, d), lambda i: (jnp.minimum((i + 1) * per, last), 0)),
                  _const_spec((1, d)),
                  _const_spec(w_up.shape),
                  _const_spec(conv_w.shape),
                  _const_spec(conv_b.shape),
                  _const_spec(w_down.shape),
                  _const_spec((1, d))],
        out_specs=pl.BlockSpec((tm, d), lambda i: (i, 0)),
        out_shape=jax.ShapeDtypeStruct((rows, d), _f32),
        compiler_params=_params("parallel"),
        name="ffn",
    )(x2d, x2d, x2d, g_pre, w_up, conv_w, conv_b, w_down, g_post)


def _rope_tables(seq):
    inv = 1.0 / (ROPE_THETA ** (jnp.arange(0, HEAD_DIM, 2, dtype=_f32) / HEAD_DIM))
    ang = jnp.arange(seq, dtype=_f32)[:, None] * inv[None, :]
    cos, sin = jnp.cos(ang), jnp.sin(ang)
    reps = LANES // HEAD_DIM
    cos_t = jnp.tile(jnp.concatenate([cos, cos], axis=1), (1, reps))
    sin_t = jnp.tile(jnp.concatenate([-sin, sin], axis=1), (1, reps))
    return cos_t, sin_t


def _trunk(x, mem, p, depth):
    batch, seq, d = x.shape
    cos, sin = _rope_tables(seq)
    x2d = x.reshape(batch * seq, d)
    row = lambda v: v.reshape(1, -1)
    for l in range(depth):
        lambda_init = 0.8 - 0.6 * math.exp(-0.3 * l)
        qat, ka, vat, qd, kd, vd = _in_proj(x2d, row(p['mix_pre_g'][l]), p['w_in'][l],
                                            cos, sin, batch, seq)
        lam_vecs = jnp.stack([p['lambda_q1'][l], p['lambda_k1'][l],
                              p['lambda_q2'][l], p['lambda_k2'][l]])
        oa = _diff_attn(qat, ka, vat, lam_vecs, row(p['diff_subln_g'][l]), lambda_init,
                        batch, seq)
        dil = [_dil_attn(qd, kd, vd, r, batch, seq) for (_, r) in DIL_PATTERNS]
        kv = _mem_kv(mem, row(p['mem_norm_g'][l]), p['w_xkv'][l])
        x2d = _mid(x2d, oa, dil, kv, p['w_out'][l], p['w_xq'][l], p['w_xo'][l],
                   row(p['mix_post_g'][l]), row(p['xattn_pre_g'][l]),
                   row(p['xattn_post_g'][l]), seq)
        x2d = _ffn(x2d, row(p['ffn_pre_g'][l]), p['w_up'][l], p['conv_w'][l],
                   row(p['conv_b'][l]), p['w_down'][l], row(p['ffn_post_g'][l]), seq)
    return x2d.reshape(batch, seq, d)


def kernel(x_prompt, x_sample, mem_prompt, mem_sample, w_in, w_out, lambda_q1, lambda_k1, lambda_q2, lambda_k2, diff_subln_g, w_xq, w_xkv, w_xo, w_up, conv_w, conv_b, w_down, mix_pre_g, mix_post_g, mem_norm_g, xattn_pre_g, xattn_post_g, ffn_pre_g, ffn_post_g):
    depth = w_in.shape[0]
    p = {
        'w_in': w_in.astype(_bf16), 'w_out': w_out.astype(_bf16),
        'lambda_q1': lambda_q1, 'lambda_k1': lambda_k1,
        'lambda_q2': lambda_q2, 'lambda_k2': lambda_k2,
        'diff_subln_g': diff_subln_g,
        'w_xq': w_xq.astype(_bf16), 'w_xkv': w_xkv.astype(_bf16), 'w_xo': w_xo.astype(_bf16),
        'w_up': w_up.astype(_bf16), 'conv_w': conv_w, 'conv_b': conv_b,
        'w_down': w_down.astype(_bf16),
        'mix_pre_g': mix_pre_g, 'mix_post_g': mix_post_g, 'mem_norm_g': mem_norm_g,
        'xattn_pre_g': xattn_pre_g, 'xattn_post_g': xattn_post_g,
        'ffn_pre_g': ffn_pre_g, 'ffn_post_g': ffn_post_g,
    }
    y_prompt = _trunk(x_prompt, mem_prompt, p, depth)
    y_sample = _trunk(x_sample, mem_sample, p, depth)
    return (y_prompt, y_sample)
```
